```python
import math, functools
import jax, jax.numpy as jnp
from jax import lax
import numpy as np

D_MODEL = 2048
BATCH = 1
SEQ = 8192
DEPTH = 1
DEC_BATCH = 128
DEC_SEQ = 8
PAST_LEN = 8192
PAGE_SIZE = 128

HEAD_DIM = 64
N_Q_HEADS = 16
N_KV_HEADS = 4
GROUP = N_Q_HEADS // N_KV_HEADS
ATTN_WIDTH = N_Q_HEADS * HEAD_DIM
KV_WIDTH = N_KV_HEADS * HEAD_DIM
WINDOW = 128
BLOCK = 128
CONV_CH = D_MODEL // 2
CONV_WIDTH = 31
D_FF = 256 * (-(-8 * D_MODEL // (3 * 256)))
ROPE_THETA = 10000.0
LN_EPS = 1e-5
ALPHA = (2 * DEPTH) ** 0.25
BETA = (8 * DEPTH) ** -0.25
ATTN_SCALE = HEAD_DIM ** -0.5
NEG = -1e30
N_IN = ATTN_WIDTH + 2 * KV_WIDTH + 2 * CONV_CH + 2 * D_MODEL
SPLITS = tuple(np.cumsum([ATTN_WIDTH, KV_WIDTH, KV_WIDTH, CONV_CH, CONV_CH, D_MODEL]).tolist())

kernel_name = "hybrid_swa_sink_conformer_conv_deepnorm_adaln_step"


def layer_norm(x, gain=None, bias=None):
    xf = x.astype(jnp.float32)
    mu = jnp.mean(xf, axis=-1, keepdims=True)
    var = jnp.mean(jnp.square(xf - mu), axis=-1, keepdims=True)
    y = (xf - mu) * lax.rsqrt(var + LN_EPS)
    if gain is not None:
        y = y * gain.astype(jnp.float32) + bias.astype(jnp.float32)
    return y.astype(x.dtype)


def rope(x, pos):
    half = HEAD_DIM // 2
    inv = ROPE_THETA ** (-jnp.arange(half, dtype=jnp.float32) / half)
    ang = pos.astype(jnp.float32)[:, None] * inv[None, :]
    cos = jnp.cos(ang)[:, None, :]
    sin = jnp.sin(ang)[:, None, :]
    xf = x.astype(jnp.float32)
    x1, x2 = xf[..., :half], xf[..., half:]
    return jnp.concatenate([x1 * cos - x2 * sin, x2 * cos + x1 * sin], axis=-1).astype(x.dtype)


def sink_softmax(s, sinks):
    sk = sinks.astype(jnp.float32).reshape(N_KV_HEADS, GROUP)[:, :, None, None]
    sk = jnp.broadcast_to(sk, s.shape[:-1] + (1,))
    m = jnp.maximum(jnp.max(s, axis=-1, keepdims=True), sk)
    p = jnp.exp(s - m)
    return p / (jnp.sum(p, axis=-1, keepdims=True) + jnp.exp(sk - m))


def attn_prompt(q, k, v, sinks):
    B, S = q.shape[0], q.shape[1]
    nb = S // BLOCK
    qb = q.reshape(B, nb, BLOCK, N_KV_HEADS, GROUP, HEAD_DIM)

    def with_prev(t):
        tb = t.reshape(B, nb, BLOCK, N_KV_HEADS, HEAD_DIM)
        prev = jnp.concatenate([jnp.zeros_like(tb[:, :1]), tb[:, :-1]], axis=1)
        return jnp.concatenate([prev, tb], axis=2)

    kk, vv = with_prev(k), with_prev(v)
    s = jnp.einsum('bnqhgd,bnkhd->bnhgqk', qb, kk, preferred_element_type=jnp.float32) * ATTN_SCALE
    qi = jnp.arange(BLOCK)[:, None] + BLOCK
    kj = jnp.arange(2 * BLOCK)[None, :]
    diff = qi - kj
    blk = jnp.arange(nb)[:, None, None]
    valid = (diff >= 0) & (diff < WINDOW) & (blk * BLOCK + kj - BLOCK >= 0)
    p = sink_softmax(jnp.where(valid[None, :, None, None], s, NEG), sinks)
    o = jnp.einsum('bnhgqk,bnkhd->bnqhgd', p.astype(vv.dtype), vv)
    n = min(WINDOW, S)
    return o.reshape(B, S, ATTN_WIDTH), k[:, S - n:], v[:, S - n:]


def attn_sample(q, k, v, sinks, k_buf, v_buf):
    Bd, T = q.shape[0], q.shape[1]
    nbuf = k_buf.shape[1]
    kk = jnp.concatenate([k_buf, k], axis=1)
    vv = jnp.concatenate([v_buf, v], axis=1)
    qg = q.reshape(Bd, T, N_KV_HEADS, GROUP, HEAD_DIM)
    s = jnp.einsum('bqhgd,bkhd->bhgqk', qg, kk, preferred_element_type=jnp.float32) * ATTN_SCALE
    qpos = PAST_LEN + jnp.arange(T)
    kpos = PAST_LEN - nbuf + jnp.arange(nbuf + T)
    diff = qpos[:, None] - kpos[None, :]
    valid = (diff >= 0) & (diff < WINDOW) & (kpos[None, :] >= 0)
    p = sink_softmax(jnp.where(valid[None, None, None], s, NEG), sinks)
    o = jnp.einsum('bhgqk,bkhd->bqhgd', p.astype(vv.dtype), vv)
    return o.reshape(Bd, T, ATTN_WIDTH), kk[:, -nbuf:], vv[:, -nbuf:]


def causal_depthwise(x_ext, w, b):
    out = lax.conv_general_dilated(x_ext, w[:, None, :].astype(x_ext.dtype), window_strides=(1,), padding='VALID',
                                   dimension_numbers=('NWC', 'WIO', 'NWC'), feature_group_count=x_ext.shape[-1])
    return out + b


def decoder_layer(x, c, pos, attend, conv_prev, w_ada, b_ada, w_in, sinks, w_dw, b_dw, cn_gain, cn_bias,
                  w_br_attn, w_br_conv, w_out, ln1_gain, ln1_bias, w_ffn_gate, w_ffn_up, w_ffn_down,
                  ln2_gain, ln2_bias):
    B, S = x.shape[0], x.shape[1]
    mod = jnp.dot(jax.nn.silu(c), w_ada) + b_ada
    sh1, sc1, g1, sh2, sc2, g2 = jnp.split(mod[:, None, :], 6, axis=-1)
    u = layer_norm(x) * (1 + sc1) + sh1
    proj = jnp.dot(u, w_in)
    q, k, v, glu_a, glu_b, gate_a, gate_b = jnp.split(proj, SPLITS, axis=-1)
    q = rope(q.reshape(B, S, N_Q_HEADS, HEAD_DIM), pos)
    k = rope(k.reshape(B, S, N_KV_HEADS, HEAD_DIM), pos)
    v = v.reshape(B, S, N_KV_HEADS, HEAD_DIM)
    attn, k_state, v_state = attend(q, k, v, sinks)
    glu = glu_a * jax.nn.sigmoid(glu_b)
    conv_in = jnp.concatenate([conv_prev, glu], axis=1)
    conv = jax.nn.silu(layer_norm(causal_depthwise(conv_in, w_dw, b_dw), cn_gain, cn_bias))
    merged = jax.nn.sigmoid(gate_a) * jnp.dot(attn, w_br_attn) + jax.nn.sigmoid(gate_b) * jnp.dot(conv, w_br_conv)
    mix = jnp.dot(merged, w_out)
    x = layer_norm(ALPHA * x + g1 * mix, ln1_gain, ln1_bias)
    u2 = layer_norm(x) * (1 + sc2) + sh2
    h = jax.nn.silu(jnp.dot(u2, w_ffn_gate)) * jnp.dot(u2, w_ffn_up)
    x = layer_norm(ALPHA * x + g2 * jnp.dot(h, w_ffn_down), ln2_gain, ln2_bias)
    conv_state = conv_in[:, conv_in.shape[1] - (CONV_WIDTH - 1):]
    return x, k_state, v_state, conv_state


def setup_inputs(seed: int = 0) -> dict:
    key = jax.random.key(seed)
    ks = jax.random.split(key, 26)
    nbuf = min(WINDOW, PAST_LEN)
    f32 = jnp.float32

    def nrm(k, shape, scale):
        return jax.random.normal(k, shape, f32) * scale

    return {
        "x_prompt": nrm(ks[0], (BATCH, SEQ, D_MODEL), 1.0),
        "x_sample": nrm(ks[1], (DEC_BATCH, DEC_SEQ, D_MODEL), 1.0),
        "c_prompt": nrm(ks[2], (BATCH, D_MODEL), 1.0),
        "c_sample": nrm(ks[3], (DEC_BATCH, D_MODEL), 1.0),
        "cache_k_win": nrm(ks[4], (DEPTH, DEC_BATCH, nbuf, N_KV_HEADS, HEAD_DIM), 1.0),
        "cache_v_win": nrm(ks[5], (DEPTH, DEC_BATCH, nbuf, N_KV_HEADS, HEAD_DIM), 1.0),
        "state_conv": nrm(ks[6], (DEPTH, DEC_BATCH, CONV_WIDTH - 1, CONV_CH), 0.5),
        "w_ada": nrm(ks[7], (DEPTH, D_MODEL, 6 * D_MODEL), 0.5 * D_MODEL ** -0.5),
        "b_ada": nrm(ks[8], (DEPTH, 6 * D_MODEL), 0.1),
        "w_in": nrm(ks[9], (DEPTH, D_MODEL, N_IN), D_MODEL ** -0.5),
        "attn_sinks": nrm(ks[10], (DEPTH, N_Q_HEADS), 0.5),
        "w_dw": nrm(ks[11], (DEPTH, CONV_WIDTH, CONV_CH), CONV_WIDTH ** -0.5),
        "b_dw": nrm(ks[12], (DEPTH, CONV_CH), 0.01),
        "cn_gain": 1.0 + nrm(ks[13], (DEPTH, CONV_CH), 0.01),
        "cn_bias": nrm(ks[14], (DEPTH, CONV_CH), 0.01),
        "w_br_attn": nrm(ks[15], (DEPTH, ATTN_WIDTH, D_MODEL), BETA * ATTN_WIDTH ** -0.5),
        "w_br_conv": nrm(ks[16], (DEPTH, CONV_CH, D_MODEL), BETA * CONV_CH ** -0.5),
        "w_out": nrm(ks[17], (DEPTH, D_MODEL, D_MODEL), BETA * D_MODEL ** -0.5),
        "ln1_gain": 1.0 + nrm(ks[18], (DEPTH, D_MODEL), 0.01),
        "ln1_bias": nrm(ks[19], (DEPTH, D_MODEL), 0.01),
        "w_ffn_gate": nrm(ks[20], (DEPTH, D_MODEL, D_FF), D_MODEL ** -0.5),
        "w_ffn_up": nrm(ks[21], (DEPTH, D_MODEL, D_FF), D_MODEL ** -0.5),
        "w_ffn_down": nrm(ks[22], (DEPTH, D_FF, D_MODEL), BETA * D_FF ** -0.5),
        "ln2_gain": 1.0 + nrm(ks[23], (DEPTH, D_MODEL), 0.01),
        "ln2_bias": nrm(ks[24], (DEPTH, D_MODEL), 0.01),
    }


def reference(x_prompt, x_sample, c_prompt, c_sample, cache_k_win, cache_v_win, state_conv,
              w_ada, b_ada, w_in, attn_sinks, w_dw, b_dw, cn_gain, cn_bias, w_br_attn, w_br_conv, w_out,
              ln1_gain, ln1_bias, w_ffn_gate, w_ffn_up, w_ffn_down, ln2_gain, ln2_bias):
    S = x_prompt.shape[1]
    T = x_sample.shape[1]
    pos_p = jnp.arange(S, dtype=jnp.int32)
    pos_s = PAST_LEN + jnp.arange(T, dtype=jnp.int32)
    yp, ys = x_prompt, x_sample
    kp_l, vp_l, cp_l, ks_l, vs_l, cs_l = [], [], [], [], [], []
    for l in range(DEPTH):
        w = (w_ada[l], b_ada[l], w_in[l], attn_sinks[l], w_dw[l], b_dw[l], cn_gain[l], cn_bias[l],
             w_br_attn[l], w_br_conv[l], w_out[l], ln1_gain[l], ln1_bias[l], w_ffn_gate[l], w_ffn_up[l],
             w_ffn_down[l], ln2_gain[l], ln2_bias[l])
        conv0 = jnp.zeros((yp.shape[0], CONV_WIDTH - 1, CONV_CH), yp.dtype)
        yp, kp, vp, cp = decoder_layer(yp, c_prompt, pos_p, attn_prompt, conv0, *w)
        att_s = functools.partial(attn_sample, k_buf=cache_k_win[l], v_buf=cache_v_win[l])
        ys, kn, vn, cn = decoder_layer(ys, c_sample, pos_s, att_s, state_conv[l], *w)
        kp_l.append(kp); vp_l.append(vp); cp_l.append(cp)
        ks_l.append(kn); vs_l.append(vn); cs_l.append(cn)
    k_win_prompt = jnp.stack(kp_l)
    v_win_prompt = jnp.stack(vp_l)
    conv_prompt = jnp.stack(cp_l)
    k_win_sample = jnp.stack(ks_l)
    v_win_sample = jnp.stack(vs_l)
    conv_sample = jnp.stack(cs_l)
    return (yp, ys, k_win_prompt, v_win_prompt, conv_prompt, k_win_sample, v_win_sample, conv_sample)
```

```python
import functools

import jax
import jax.numpy as jnp
from jax import lax
from jax.experimental import pallas as pl
from jax.experimental.pallas import tpu as pltpu

F32 = jnp.float32
BF16 = jnp.bfloat16

D_MODEL = 2048
HEAD_DIM = 64
N_Q_HEADS = 16
N_KV_HEADS = 4
ATTN_WIDTH = N_Q_HEADS * HEAD_DIM
KV_WIDTH = N_KV_HEADS * HEAD_DIM
WINDOW = 128
CONV_CH = D_MODEL // 2
CONV_WIDTH = 31
CONV_HIST = CONV_WIDTH - 1
D_FF = 5632
ROPE_THETA = 10000.0
LN_EPS = 1e-5
NEG = -1e30
ATTN_SCALE = HEAD_DIM ** -0.5
PAST_LEN = 8192
N_PROJ_A = ATTN_WIDTH + 2 * KV_WIDTH + 2 * CONV_CH

LANES = 128
VMEM_LIMIT = 60 * 1024 * 1024


def _params(*sem):
    return pltpu.CompilerParams(dimension_semantics=sem, vmem_limit_bytes=VMEM_LIMIT)


def _resident(shape):
    return pl.BlockSpec(shape, lambda *_: (0,) * len(shape), pipeline_mode=pl.Buffered(1))


def _ln(x):
    mu = jnp.mean(x, axis=-1, keepdims=True)
    xc = x - mu
    var = jnp.mean(xc * xc, axis=-1, keepdims=True)
    return xc * lax.rsqrt(var + LN_EPS)


def _silu(x):
    return x * jax.nn.sigmoid(x)


class _Rows:
    def __init__(self, B, T, tm):
        self.B, self.T, self.tm = B, T, tm
        if B == 1:
            self.bb, self.tt, self.n = 1, tm, T // tm
            self.xmap = lambda i, *_: (0, i, 0)
            self._mrow = lambda i: 0
        else:
            self.bb, self.tt, self.n = tm // T, T, (B * T) // tm
            self.xmap = lambda i, *_: (i, 0, 0)
            self._mrow = lambda i: i
        self.rows = B * T

    def x_spec(self, width=D_MODEL):
        return pl.BlockSpec((self.bb, self.tt, width), self.xmap)

    def mod_spec(self, k):
        return pl.BlockSpec((self.bb, D_MODEL), lambda i, *_: (self._mrow(i), k))

    def row_spec(self, width, col=0):
        return pl.BlockSpec((self.tm, width), lambda i, *_: (i, col))


def _mod_body(c_ref, w_ref, b_ref, o_ref):
    a = _silu(c_ref[...]).astype(BF16)
    o_ref[...] = jnp.dot(a, w_ref[...].astype(BF16), preferred_element_type=F32) + b_ref[...]


def _adaln(c_all, w_ada, b_ada):
    R = c_all.shape[0]
    tn = 1024
    return pl.pallas_call(
        _mod_body,
        grid=(6 * D_MODEL // tn,),
        in_specs=[pl.BlockSpec((R, D_MODEL), lambda j: (0, 0)),
                  pl.BlockSpec((D_MODEL, tn), lambda j: (0, j)),
                  pl.BlockSpec((1, tn), lambda j: (0, j))],
        out_specs=pl.BlockSpec((R, tn), lambda j: (0, j)),
        out_shape=jax.ShapeDtypeStruct((R, 6 * D_MODEL), F32),
        compiler_params=_params("arbitrary"),
        name="adaln_mod",
    )(c_all, w_ada, b_ada.reshape(1, -1))


def _lnmod_body(x_ref, sh_ref, sc_ref, u_ref):
    x = x_ref[...]
    y = _ln(x) * (1.0 + sc_ref[...][:, None, :]) + sh_ref[...][:, None, :]
    u_ref[...] = y.reshape(u_ref.shape).astype(BF16)


def _ln_mod(x, mod, rows, k_shift, k_scale):
    return pl.pallas_call(
        _lnmod_body,
        grid=(rows.n,),
        in_specs=[rows.x_spec(), rows.mod_spec(k_shift), rows.mod_spec(k_scale)],
        out_specs=rows.row_spec(D_MODEL),
        out_shape=jax.ShapeDtypeStruct((rows.rows, D_MODEL), BF16),
        compiler_params=_params("arbitrary"),
        name="ln_mod",
    )(x, mod, mod)


def _rope_store(x, cos_t, sin_t, o_ref, col0, scale):
    lane = lax.broadcasted_iota(jnp.int32, (x.shape[0], LANES), 1)
    first_half = (lane & (HEAD_DIM // 2)) == 0
    for g in range(x.shape[1] // LANES):
        xg = x[:, g * LANES:(g + 1) * LANES]
        partner = jnp.where(first_half, pltpu.roll(xg, LANES - HEAD_DIM // 2, 1), pltpu.roll(xg, HEAD_DIM // 2, 1))
        r = xg * cos_t + partner * sin_t
        if scale != 1.0:
            r = r * scale
        o_ref[:, col0 + g * LANES:col0 + (g + 1) * LANES] = r.astype(o_ref.dtype)


def _proj_a_body(u_ref, w_ref, cos_ref, sin_ref, q_ref, kv_ref, glu_ref):
    u = u_ref[...]
    cos_t, sin_t = cos_ref[...], sin_ref[...]
    q = jnp.dot(u, w_ref[:, 0:ATTN_WIDTH], preferred_element_type=F32)
    _rope_store(q, cos_t, sin_t, q_ref, 0, ATTN_SCALE)
    kv = jnp.dot(u, w_ref[:, ATTN_WIDTH:ATTN_WIDTH + 2 * KV_WIDTH], preferred_element_type=F32)
    _rope_store(kv[:, 0:KV_WIDTH], cos_t, sin_t, kv_ref, 0, 1.0)
    kv_ref[:, KV_WIDTH:2 * KV_WIDTH] = kv[:, KV_WIDTH:2 * KV_WIDTH]
    c0 = ATTN_WIDTH + 2 * KV_WIDTH
    a = jnp.dot(u, w_ref[:, c0:c0 + CONV_CH], preferred_element_type=F32)
    b = jnp.dot(u, w_ref[:, c0 + CONV_CH:c0 + 2 * CONV_CH], preferred_element_type=F32)
    glu_ref[...] = a * jax.nn.sigmoid(b)


def _proj_a(u, w_a, cos_t, sin_t, tm, q_dtype):
    R = u.shape[0]
    row = lambda w: pl.BlockSpec((tm, w), lambda i: (i, 0))
    return pl.pallas_call(
        _proj_a_body,
        grid=(R // tm,),
        in_specs=[row(D_MODEL), _resident((D_MODEL, N_PROJ_A)), row(LANES), row(LANES)],
        out_specs=[row(ATTN_WIDTH), row(2 * KV_WIDTH), row(CONV_CH)],
        out_shape=[jax.ShapeDtypeStruct((R, ATTN_WIDTH), q_dtype),
                   jax.ShapeDtypeStruct((R, 2 * KV_WIDTH), F32),
                   jax.ShapeDtypeStruct((R, CONV_CH), F32)],
        compiler_params=_params("arbitrary"),
        name="proj_qkv_glu",
    )(u, w_a, cos_t, sin_t)


def _proj_b_body(u_ref, w_ref, o_ref):
    u = u_ref[...]
    cw = 1024
    for c in range(o_ref.shape[1] // cw):
        z = jnp.dot(u, w_ref[:, c * cw:(c + 1) * cw], preferred_element_type=F32)
        o_ref[:, c * cw:(c + 1) * cw] = jax.nn.sigmoid(z).astype(o_ref.dtype)


def _proj_b(u, w_g, tm):
    R = u.shape[0]
    N = w_g.shape[1]
    return pl.pallas_call(
        _proj_b_body,
        grid=(R // tm,),
        in_specs=[pl.BlockSpec((tm, D_MODEL), lambda i: (i, 0)), _resident((D_MODEL, N))],
        out_specs=pl.BlockSpec((tm, N), lambda i: (i, 0)),
        out_shape=jax.ShapeDtypeStruct((R, N), BF16),
        compiler_params=_params("arbitrary"),
        name="proj_gates",
    )(u, w_g)


def _attend(sink_ref, q_tile, k_all, v_all, valid, store):
    nk = k_all.shape[0]
    lane = lax.broadcasted_iota(jnp.int32, (nk, LANES), 1)
    low = lane < HEAD_DIM
    out_low = lax.broadcasted_iota(jnp.int32, (valid.shape[0], LANES), 1) < HEAD_DIM
    for j in range(KV_WIDTH // LANES):
        kt = k_all[:, j * LANES:(j + 1) * LANES]
        vt = v_all[:, j * LANES:(j + 1) * LANES]
        for gg in range(2):
            g = 2 * j + gg
            keep = low if gg == 0 else jnp.logical_not(low)
            kb = jnp.where(keep, kt, 0.0)
            vb = jnp.where(keep, vt, 0.0)
            ko = pltpu.roll(kb, HEAD_DIM, 1)
            vo = pltpu.roll(vb, HEAD_DIM, 1)
            k_lo, k_hi = (kb, ko) if gg == 0 else (ko, kb)
            v_lo, v_hi = (vb, vo) if gg == 0 else (vo, vb)
            kk = jnp.concatenate([k_lo, k_hi], axis=0).astype(BF16)
            vv = jnp.concatenate([v_lo, v_hi], axis=0).astype(BF16)
            for i in range(2):
                t = 2 * g + i
                s = lax.dot_general(q_tile(t), kk, (((1,), (1,)), ((), ())), preferred_element_type=F32)
                ps, inv = [], []
                for half in range(2):
                    sink = sink_ref[4 * g + 2 * i + half]
                    sh = jnp.where(valid, s[:, half * nk:(half + 1) * nk], NEG)
                    m = jnp.maximum(jnp.max(sh, axis=-1, keepdims=True), sink)
                    p = jnp.exp(sh - m)
                    den = jnp.sum(p, axis=-1, keepdims=True) + jnp.exp(sink - m)
                    ps.append(p.astype(BF16))
                    inv.append(1.0 / den)
                o = jnp.dot(jnp.concatenate(ps, axis=1), vv, preferred_element_type=F32)
                store(t, o * jnp.where(out_low, inv[0], inv[1]))


def _attn_prompt_body(sink_ref, q_ref, kvp_ref, kvc_ref, o_ref):
    n = pl.program_id(0)
    row = lax.broadcasted_iota(jnp.int32, (WINDOW, 2 * WINDOW), 0)
    col = lax.broadcasted_iota(jnp.int32, (WINDOW, 2 * WINDOW), 1)
    valid = (col > row) & (col <= row + WINDOW) & ((col >= WINDOW) | (n > 0))
    k_all = jnp.concatenate([kvp_ref[:, 0:KV_WIDTH], kvc_ref[:, 0:KV_WIDTH]], axis=0)
    v_all = jnp.concatenate([kvp_ref[:, KV_WIDTH:], kvc_ref[:, KV_WIDTH:]], axis=0)

    def store(t, o):
        o_ref[:, t * LANES:(t + 1) * LANES] = o.astype(o_ref.dtype)

    _attend(sink_ref, lambda t: q_ref[:, t * LANES:(t + 1) * LANES], k_all, v_all, valid, store)


def _attn_prompt(sinks, q, kv):
    S = q.shape[0]
    nb = S // WINDOW
    return pl.pallas_call(
        _attn_prompt_body,
        grid=(nb,),
        in_specs=[pl.BlockSpec(memory_space=pltpu.SMEM),
                  pl.BlockSpec((WINDOW, ATTN_WIDTH), lambda n: (n, 0)),
                  pl.BlockSpec((WINDOW, 2 * KV_WIDTH), lambda n: (jnp.maximum(n - 1, 0), 0)),
                  pl.BlockSpec((WINDOW, 2 * KV_WIDTH), lambda n: (n, 0))],
        out_specs=pl.BlockSpec((WINDOW, ATTN_WIDTH), lambda n: (n, 0)),
        out_shape=jax.ShapeDtypeStruct((S, ATTN_WIDTH), BF16),
        compiler_params=_params("arbitrary"),
        name="attn_prompt",
    )(sinks, q, kv, kv)


def _attn_sample_body(sink_ref, q_ref, kvn_ref, ck_ref, cv_ref, o_ref, kw_ref, vw_ref):
    bb, T = q_ref.shape[0], q_ref.shape[1]
    nbuf = ck_ref.shape[1]
    row = lax.broadcasted_iota(jnp.int32, (T, 2 * WINDOW), 0)
    col = lax.broadcasted_iota(jnp.int32, (T, 2 * WINDOW), 1)
    valid = (col > row + nbuf - WINDOW) & (col <= row + nbuf)
    pad = jnp.zeros((2 * WINDOW - nbuf - T, KV_WIDTH), F32)

    def one_seq(b, carry):
        kb, vb = ck_ref[b], cv_ref[b]
        kn, vn = kvn_ref[b][:, 0:KV_WIDTH], kvn_ref[b][:, KV_WIDTH:]
        kw_ref[b, 0:nbuf - T, :] = kb[T:nbuf]
        kw_ref[b, nbuf - T:nbuf, :] = kn
        vw_ref[b, 0:nbuf - T, :] = vb[T:nbuf]
        vw_ref[b, nbuf - T:nbuf, :] = vn
        k_all = jnp.concatenate([kb, kn, pad], axis=0)
        v_all = jnp.concatenate([vb, vn, pad], axis=0)
        qb = q_ref[b]

        def store(t, o):
            o_ref[b, :, t * LANES:(t + 1) * LANES] = o

        _attend(sink_ref, lambda t: qb[:, t * LANES:(t + 1) * LANES].astype(BF16), k_all, v_all, valid, store)
        return carry

    lax.fori_loop(0, bb, one_seq, 0)


def _attn_sample(sinks, q3, kvn3, ck, cv, bb):
    Bd, T = q3.shape[0], q3.shape[1]
    nbuf = ck.shape[1]
    assert nbuf == WINDOW and T % 8 == 0 and nbuf + T <= 2 * WINDOW
    blk = lambda r, w: pl.BlockSpec((bb, r, w), lambda i: (i, 0, 0))
    return pl.pallas_call(
        _attn_sample_body,
        grid=(Bd // bb,),
        in_specs=[pl.BlockSpec(memory_space=pltpu.SMEM), blk(T, ATTN_WIDTH), blk(T, 2 * KV_WIDTH),
                  blk(nbuf, KV_WIDTH), blk(nbuf, KV_WIDTH)],
        out_specs=[blk(T, ATTN_WIDTH), blk(nbuf, KV_WIDTH), blk(nbuf, KV_WIDTH)],
        out_shape=[jax.ShapeDtypeStruct((Bd, T, ATTN_WIDTH), F32),
                   jax.ShapeDtypeStruct((Bd, nbuf, KV_WIDTH), F32),
                   jax.ShapeDtypeStruct((Bd, nbuf, KV_WIDTH), F32)],
        compiler_params=_params("arbitrary"),
        name="attn_sample",
    )(sinks, q3, kvn3, ck, cv)


CONV_HALO = 32
CONV_ROWS = 32


def _conv_post(acc, gain, bias):
    return _silu(_ln(acc) * gain + bias)


def _conv_prompt_body(cur_ref, halo_ref, w_ref, b_ref, g_ref, beta_ref, o_ref, ext_ref):
    i = pl.program_id(0)
    tm = cur_ref.shape[0]
    ext_ref[0:CONV_HALO, :] = jnp.where(i > 0, halo_ref[...], 0.0)
    ext_ref[CONV_HALO:, :] = cur_ref[...]
    off = CONV_HALO - CONV_HIST
    for r in range(tm // CONV_ROWS):
        acc = jnp.zeros((CONV_ROWS, CONV_CH), F32) + b_ref[...]
        for j in range(CONV_WIDTH):
            s = r * CONV_ROWS + off + j
            acc = acc + w_ref[j:j + 1, :] * ext_ref[s:s + CONV_ROWS, :]
        y = _conv_post(acc, g_ref[...], beta_ref[...])
        o_ref[r * CONV_ROWS:(r + 1) * CONV_ROWS, :] = y.astype(o_ref.dtype)


def _conv_prompt(glu, w_dw, b_dw, gain, bias, tm):
    S = glu.shape[0]
    hb = tm // CONV_HALO
    vec = lambda: pl.BlockSpec((1, CONV_CH), lambda i: (0, 0))
    return pl.pallas_call(
        _conv_prompt_body,
        grid=(S // tm,),
        in_specs=[pl.BlockSpec((tm, CONV_CH), lambda i: (i, 0)),
                  pl.BlockSpec((CONV_HALO, CONV_CH), lambda i: (jnp.maximum(i * hb - 1, 0), 0)),
                  pl.BlockSpec((CONV_WIDTH, CONV_CH), lambda i: (0, 0)), vec(), vec(), vec()],
        out_specs=pl.BlockSpec((tm, CONV_CH), lambda i: (i, 0)),
        out_shape=jax.ShapeDtypeStruct((S, CONV_CH), BF16),
        scratch_shapes=[pltpu.VMEM((tm + CONV_HALO, CONV_CH), F32)],
        compiler_params=_params("arbitrary"),
        name="conv_prompt",
    )(glu, glu, w_dw, b_dw.reshape(1, -1), gain.reshape(1, -1), bias.reshape(1, -1))


def _conv_sample_body(st_ref, glu_ref, w_ref, b_ref, g_ref, beta_ref, o_ref, ns_ref, ext_ref):
    T = glu_ref.shape[1]
    off = CONV_HALO - CONV_HIST
    ext_ref[:, off:CONV_HALO, :] = st_ref[...]
    ext_ref[:, CONV_HALO:CONV_HALO + T, :] = glu_ref[...]
    acc = jnp.zeros(o_ref.shape, F32) + b_ref[...]
    for j in range(CONV_WIDTH):
        acc = acc + w_ref[j:j + 1, :] * ext_ref[:, off + j:off + j + T, :]
    o_ref[...] = _conv_post(acc, g_ref[...], beta_ref[...])
    ns_ref[...] = ext_ref[:, off + T:CONV_HALO + T, :]


def _conv_sample(state, glu3, w_dw, b_dw, gain, bias, bb):
    Bd, T = glu3.shape[0], glu3.shape[1]
    vec = lambda: pl.BlockSpec((1, CONV_CH), lambda i: (0, 0))
    blk = lambda r: pl.BlockSpec((bb, r, CONV_CH), lambda i: (i, 0, 0))
    return pl.pallas_call(
        _conv_sample_body,
        grid=(Bd // bb,),
        in_specs=[blk(CONV_HIST), blk(T), pl.BlockSpec((CONV_WIDTH, CONV_CH), lambda i: (0, 0)), vec(), vec(), vec()],
        out_specs=[blk(T), blk(CONV_HIST)],
        out_shape=[jax.ShapeDtypeStruct((Bd, T, CONV_CH), F32),
                   jax.ShapeDtypeStruct((Bd, CONV_HIST, CONV_CH), F32)],
        scratch_shapes=[pltpu.VMEM((bb, CONV_HALO + T, CONV_CH), F32)],
        compiler_params=_params("arbitrary"),
        name="conv_sample",
    )(state, glu3, w_dw, b_dw.reshape(1, -1), gain.reshape(1, -1), bias.reshape(1, -1))


def _mix_body(alpha, attn_ref, conv_ref, ga_ref, gb_ref, x_ref, g1_ref, sh2_ref, sc2_ref, gain_ref, bias_ref,
              wa_ref, wc_ref, wo_ref, x1_ref, u2_ref):
    a = jnp.dot(attn_ref[...].astype(BF16), wa_ref[...], preferred_element_type=F32)
    c = jnp.dot(conv_ref[...].astype(BF16), wc_ref[...], preferred_element_type=F32)
    merged = ga_ref[...].astype(F32) * a + gb_ref[...].astype(F32) * c
    mix = jnp.dot(merged.astype(BF16), wo_ref[...], preferred_element_type=F32)
    x = x_ref[...]
    y = alpha * x + g1_ref[...][:, None, :] * mix.reshape(x.shape)
    x1 = _ln(y) * gain_ref[...] + bias_ref[...]
    x1_ref[...] = x1
    u2 = _ln(x1) * (1.0 + sc2_ref[...][:, None, :]) + sh2_ref[...][:, None, :]
    u2_ref[...] = u2.reshape(u2_ref.shape).astype(BF16)


def _mix(attn, conv, gates, x, mod, ln_gain, ln_bias, w_a, w_c, w_o, rows, alpha):
    vec = lambda: pl.BlockSpec((1, D_MODEL), lambda i: (0, 0))
    return pl.pallas_call(
        functools.partial(_mix_body, alpha),
        grid=(rows.n,),
        in_specs=[rows.row_spec(ATTN_WIDTH), rows.row_spec(CONV_CH), rows.row_spec(D_MODEL, 0), rows.row_spec(D_MODEL, 1),
                  rows.x_spec(), rows.mod_spec(2), rows.mod_spec(3), rows.mod_spec(4), vec(), vec(),
                  _resident((ATTN_WIDTH, D_MODEL)), _resident((CONV_CH, D_MODEL)), _resident((D_MODEL, D_MODEL))],
        out_specs=[rows.x_spec(), rows.row_spec(D_MODEL)],
        out_shape=[jax.ShapeDtypeStruct(x.shape, F32), jax.ShapeDtypeStruct((rows.rows, D_MODEL), BF16)],
        compiler_params=_params("arbitrary"),
        name="merge_out_ln1",
    )(attn, conv, gates, gates, x, mod, mod, mod, ln_gain.reshape(1, -1), ln_bias.reshape(1, -1), w_a, w_c, w_o)


def _ffn_body(alpha, u_ref, x1_ref, g2_ref, gain_ref, bias_ref, wg_ref, wu_ref, wd_ref, y_ref, acc_ref):
    k = pl.program_id(1)
    u = u_ref[...]
    g = jnp.dot(u, wg_ref[...], preferred_element_type=F32)
    up = jnp.dot(u, wu_ref[...], preferred_element_type=F32)
    h = (_silu(g) * up).astype(BF16)
    d = jnp.dot(h, wd_ref[...], preferred_element_type=F32)

    @pl.when(k == 0)
    def _():
        acc_ref[...] = d

    @pl.when(k > 0)
    def _():
        acc_ref[...] += d

    @pl.when(k == pl.num_programs(1) - 1)
    def _():
        x1 = x1_ref[...]
        y = alpha * x1 + g2_ref[...][:, None, :] * acc_ref[...].reshape(x1.shape)
        y_ref[...] = _ln(y) * gain_ref[...] + bias_ref[...]


def _ffn(u2, x1, mod, ln_gain, ln_bias, w_g, w_u, w_d, rows, tf, alpha):
    vec = lambda: pl.BlockSpec((1, D_MODEL), lambda i, k: (0, 0))
    return pl.pallas_call(
        functools.partial(_ffn_body, alpha),
        grid=(rows.n, D_FF // tf),
        in_specs=[rows.row_spec(D_MODEL), rows.x_spec(), rows.mod_spec(5), vec(), vec(),
                  pl.BlockSpec((D_MODEL, tf), lambda i, k: (0, k)),
                  pl.BlockSpec((D_MODEL, tf), lambda i, k: (0, k)),
                  pl.BlockSpec((tf, D_MODEL), lambda i, k: (k, 0))],
        out_specs=rows.x_spec(),
        out_shape=jax.ShapeDtypeStruct(x1.shape, F32),
        scratch_shapes=[pltpu.VMEM((rows.tm, D_MODEL), F32)],
        compiler_params=_params("arbitrary", "arbitrary"),
        name="ffn_ln2",
    )(u2, x1, mod, ln_gain.reshape(1, -1), ln_bias.reshape(1, -1), w_g, w_u, w_d)


def _rope_tables(pos):
    half = HEAD_DIM // 2
    inv = ROPE_THETA ** (-jnp.arange(half, dtype=F32) / half)
    ang = pos.astype(F32)[:, None] * inv[None, :]
    cos, sin = jnp.cos(ang), jnp.sin(ang)
    reps = LANES // HEAD_DIM
    return jnp.tile(jnp.concatenate([cos, cos], axis=1), (1, reps)), jnp.tile(jnp.concatenate([-sin, sin], axis=1), (1, reps))


TM_LN = 512
TM_PROJ = 512
TM_MIX = 256
TM_FFN = 512
TF_FFN = 512


def _dense_front(x, mod, w, pos_rows, q_dtype):
    B, T, _ = x.shape
    u = _ln_mod(x, mod, _Rows(B, T, TM_LN), 0, 1)
    cos_t, sin_t = _rope_tables(pos_rows)
    q, kv, glu = _proj_a(u, w["in_a"], cos_t, sin_t, TM_PROJ, q_dtype)
    gates = _proj_b(u, w["in_b"], TM_PROJ)
    return q, kv, glu, gates


def _dense_back(attn, conv, gates, x, mod, w, alpha):
    B, T, _ = x.shape
    x1, u2 = _mix(attn, conv, gates, x, mod, w["ln1_gain"], w["ln1_bias"], w["br_attn"], w["br_conv"], w["out"],
                  _Rows(B, T, TM_MIX), alpha)
    return _ffn(u2, x1, mod, w["ln2_gain"], w["ln2_bias"], w["ffn_gate"], w["ffn_up"], w["ffn_down"],
                _Rows(B, T, TM_FFN), TF_FFN, alpha)


def kernel(x_prompt, x_sample, c_prompt, c_sample, cache_k_win, cache_v_win, state_conv, w_ada, b_ada, w_in, attn_sinks, w_dw, b_dw, cn_gain, cn_bias, w_br_attn, w_br_conv, w_out, ln1_gain, ln1_bias, w_ffn_gate, w_ffn_up, w_ffn_down, ln2_gain, ln2_bias):
    depth = w_in.shape[0]
    alpha = (2 * depth) ** 0.25
    B, S, _ = x_prompt.shape
    Bd, T, _ = x_sample.shape
    nbuf = cache_k_win.shape[2]
    assert B == 1 and S % WINDOW == 0 and S >= WINDOW

    pos_p = jnp.arange(S, dtype=jnp.int32)
    pos_s = jnp.tile(PAST_LEN + jnp.arange(T, dtype=jnp.int32), Bd)

    n_c = B + Bd
    pad = (-n_c) % 8
    c_all = jnp.concatenate([c_prompt, c_sample, jnp.zeros((pad, D_MODEL), F32)], axis=0)

    yp, ys = x_prompt, x_sample
    outs = [[] for _ in range(6)]
    for l in range(depth):
        w = {
            "in_a": w_in[l][:, :N_PROJ_A].astype(BF16), "in_b": w_in[l][:, N_PROJ_A:].astype(BF16),
            "br_attn": w_br_attn[l].astype(BF16), "br_conv": w_br_conv[l].astype(BF16), "out": w_out[l].astype(BF16),
            "ffn_gate": w_ffn_gate[l].astype(BF16), "ffn_up": w_ffn_up[l].astype(BF16), "ffn_down": w_ffn_down[l].astype(BF16),
            "ln1_gain": ln1_gain[l], "ln1_bias": ln1_bias[l], "ln2_gain": ln2_gain[l], "ln2_bias": ln2_bias[l],
        }
        mod = _adaln(c_all, w_ada[l], b_ada[l])
        mod_p, mod_s = mod[0:B], mod[B:n_c]
        sinks = attn_sinks[l]

        q, kv, glu, gates = _dense_front(yp, mod_p, w, pos_p, BF16)
        attn = _attn_prompt(sinks, q, kv)
        conv = _conv_prompt(glu, w_dw[l], b_dw[l], cn_gain[l], cn_bias[l], 256)
        yp = _dense_back(attn, conv, gates, yp, mod_p, w, alpha)
        n = min(WINDOW, S)
        outs[0].append(kv[S - n:, 0:KV_WIDTH].reshape(B, n, N_KV_HEADS, HEAD_DIM))
        outs[1].append(kv[S - n:, KV_WIDTH:].reshape(B, n, N_KV_HEADS, HEAD_DIM))
        outs[2].append(glu[S - CONV_HIST:].reshape(B, CONV_HIST, CONV_CH))

        q, kv, glu, gates = _dense_front(ys, mod_s, w, pos_s, F32)
        attn3, kwin, vwin = _attn_sample(sinks, q.reshape(Bd, T, ATTN_WIDTH), kv.reshape(Bd, T, 2 * KV_WIDTH),
                                         cache_k_win[l].reshape(Bd, nbuf, KV_WIDTH), cache_v_win[l].reshape(Bd, nbuf, KV_WIDTH), 16)
        conv3, nstate = _conv_sample(state_conv[l], glu.reshape(Bd, T, CONV_CH), w_dw[l], b_dw[l], cn_gain[l], cn_bias[l], 16)
        ys = _dense_back(attn3.reshape(Bd * T, ATTN_WIDTH), conv3.reshape(Bd * T, CONV_CH), gates, ys, mod_s, w, alpha)
        outs[3].append(kwin.reshape(Bd, nbuf, N_KV_HEADS, HEAD_DIM))
        outs[4].append(vwin.reshape(Bd, nbuf, N_KV_HEADS, HEAD_DIM))
        outs[5].append(nstate)

    return (yp, ys) + tuple(jnp.stack(o) for o in outs)
```

```python
import functools

import jax
import jax.numpy as jnp
from jax import lax
from jax.experimental import pallas as pl
from jax.experimental.pallas import tpu as pltpu

F32 = jnp.float32
BF16 = jnp.bfloat16

D_MODEL = 2048
HEAD_DIM = 64
N_Q_HEADS = 16
N_KV_HEADS = 4
ATTN_WIDTH = N_Q_HEADS * HEAD_DIM
KV_WIDTH = N_KV_HEADS * HEAD_DIM
WINDOW = 128
CONV_CH = D_MODEL // 2
CONV_WIDTH = 31
CONV_HIST = CONV_WIDTH - 1
D_FF = 5632
ROPE_THETA = 10000.0
LN_EPS = 1e-5
NEG = -1e30
ATTN_SCALE = HEAD_DIM ** -0.5
PAST_LEN = 8192
N_PROJ_A = ATTN_WIDTH + 2 * KV_WIDTH + 2 * CONV_CH

LANES = 128
SUBLANES = 8
VMEM_LIMIT = 60 * 1024 * 1024


def _params(*sem):
    return pltpu.CompilerParams(dimension_semantics=sem, vmem_limit_bytes=VMEM_LIMIT)


def _resident(shape):
    return pl.BlockSpec(shape, lambda *_: (0,) * len(shape), pipeline_mode=pl.Buffered(1))


def _ln(x):
    mu = jnp.mean(x, axis=-1, keepdims=True)
    xc = x - mu
    var = jnp.mean(xc * xc, axis=-1, keepdims=True)
    return xc * lax.rsqrt(var + LN_EPS)


def _silu(x):
    return x * jax.nn.sigmoid(x)


class _Rows:
    def __init__(self, B, T, tm):
        self.B, self.T, self.tm = B, T, tm
        if B == 1:
            self.bb, self.tt, self.n = 1, tm, T // tm
            self.xmap = lambda i, *_: (0, i, 0)
            self._mrow = lambda i: 0
        else:
            self.bb, self.tt, self.n = tm // T, T, (B * T) // tm
            self.xmap = lambda i, *_: (i, 0, 0)
            self._mrow = lambda i: i
        self.rows = B * T

    def x_spec(self, width=D_MODEL, single=False):
        if single:
            return pl.BlockSpec((self.bb, self.tt, width), self.xmap, pipeline_mode=pl.Buffered(1))
        return pl.BlockSpec((self.bb, self.tt, width), self.xmap)

    def mod_spec(self, k):
        return pl.BlockSpec((self.bb, D_MODEL), lambda i, *_: (self._mrow(i), k))

    def row_spec(self, width, col=0):
        return pl.BlockSpec((self.tm, width), lambda i, *_: (i, col))


def _mod_body(c_ref, w_ref, b_ref, o_ref):
    a = _silu(c_ref[...]).astype(BF16)
    o_ref[...] = jnp.dot(a, w_ref[...].astype(BF16), preferred_element_type=F32) + b_ref[...]


def _adaln(c_all, w_ada, b_ada):
    R = c_all.shape[0]
    tn = 1024
    return pl.pallas_call(
        _mod_body,
        grid=(6 * D_MODEL // tn,),
        in_specs=[pl.BlockSpec((R, D_MODEL), lambda j: (0, 0)),
                  pl.BlockSpec((D_MODEL, tn), lambda j: (0, j)),
                  pl.BlockSpec((1, tn), lambda j: (0, j))],
        out_specs=pl.BlockSpec((R, tn), lambda j: (0, j)),
        out_shape=jax.ShapeDtypeStruct((R, 6 * D_MODEL), F32),
        compiler_params=_params("arbitrary"),
        name="adaln_mod",
    )(c_all, w_ada, b_ada.reshape(1, -1))


def _lnmod_body(x_ref, sh_ref, sc_ref, u_ref):
    x = x_ref[...]
    y = _ln(x) * (1.0 + sc_ref[...][:, None, :]) + sh_ref[...][:, None, :]
    u_ref[...] = y.reshape(u_ref.shape).astype(BF16)


def _ln_mod(x, mod, rows, k_shift, k_scale):
    return pl.pallas_call(
        _lnmod_body,
        grid=(rows.n,),
        in_specs=[rows.x_spec(), rows.mod_spec(k_shift), rows.mod_spec(k_scale)],
        out_specs=rows.row_spec(D_MODEL),
        out_shape=jax.ShapeDtypeStruct((rows.rows, D_MODEL), BF16),
        compiler_params=_params("arbitrary"),
        name="ln_mod",
    )(x, mod, mod)


def _rope_store(x, cos_t, sin_t, o_ref, col0, scale):
    lane = lax.broadcasted_iota(jnp.int32, (x.shape[0], LANES), 1)
    first_half = (lane & (HEAD_DIM // 2)) == 0
    for g in range(x.shape[1] // LANES):
        xg = x[:, g * LANES:(g + 1) * LANES]
        partner = jnp.where(first_half, pltpu.roll(xg, LANES - HEAD_DIM // 2, 1), pltpu.roll(xg, HEAD_DIM // 2, 1))
        r = xg * cos_t + partner * sin_t
        if scale != 1.0:
            r = r * scale
        o_ref[:, col0 + g * LANES:col0 + (g + 1) * LANES] = r.astype(o_ref.dtype)


PROJ_CHUNK = 512


def _proj_body(u_ref, w_ref, cos_ref, sin_ref, q_ref, kv_ref, glu_ref, gates_ref):
    u = u_ref[...]
    cos_t, sin_t = cos_ref[...], sin_ref[...]

    def proj(c0, width):
        return jnp.dot(u, w_ref[:, c0:c0 + width], preferred_element_type=F32)

    for c in range(ATTN_WIDTH // PROJ_CHUNK):
        _rope_store(proj(c * PROJ_CHUNK, PROJ_CHUNK), cos_t, sin_t, q_ref, c * PROJ_CHUNK, ATTN_SCALE)
    kv = proj(ATTN_WIDTH, 2 * KV_WIDTH)
    _rope_store(kv[:, 0:KV_WIDTH], cos_t, sin_t, kv_ref, 0, 1.0)
    kv_ref[:, KV_WIDTH:2 * KV_WIDTH] = kv[:, KV_WIDTH:2 * KV_WIDTH]
    c0 = ATTN_WIDTH + 2 * KV_WIDTH
    for c in range(CONV_CH // PROJ_CHUNK):
        a = proj(c0 + c * PROJ_CHUNK, PROJ_CHUNK)
        b = proj(c0 + CONV_CH + c * PROJ_CHUNK, PROJ_CHUNK)
        glu_ref[:, c * PROJ_CHUNK:(c + 1) * PROJ_CHUNK] = a * jax.nn.sigmoid(b)
    for c in range(2 * D_MODEL // PROJ_CHUNK):
        z = proj(N_PROJ_A + c * PROJ_CHUNK, PROJ_CHUNK)
        gates_ref[:, c * PROJ_CHUNK:(c + 1) * PROJ_CHUNK] = jax.nn.sigmoid(z).astype(gates_ref.dtype)


def _proj(u, w_in, cos_t, sin_t, tm, q_dtype):
    R = u.shape[0]
    row = lambda w: pl.BlockSpec((tm, w), lambda i: (i, 0))
    return pl.pallas_call(
        _proj_body,
        grid=(R // tm,),
        in_specs=[row(D_MODEL), _resident(w_in.shape), row(LANES), row(LANES)],
        out_specs=[row(ATTN_WIDTH), row(2 * KV_WIDTH), row(CONV_CH), row(2 * D_MODEL)],
        out_shape=[jax.ShapeDtypeStruct((R, ATTN_WIDTH), q_dtype),
                   jax.ShapeDtypeStruct((R, 2 * KV_WIDTH), F32),
                   jax.ShapeDtypeStruct((R, CONV_CH), F32),
                   jax.ShapeDtypeStruct((R, 2 * D_MODEL), BF16)],
        compiler_params=_params("arbitrary"),
        name="proj_in",
    )(u, w_in, cos_t, sin_t)


def _attend(sink_ref, q_tile, k_all, v_all, valid, store):
    nk = k_all.shape[0]
    lane = lax.broadcasted_iota(jnp.int32, (nk, LANES), 1)
    low = lane < HEAD_DIM
    out_low = lax.broadcasted_iota(jnp.int32, (valid.shape[0], LANES), 1) < HEAD_DIM
    for j in range(KV_WIDTH // LANES):
        kt = k_all[:, j * LANES:(j + 1) * LANES]
        vt = v_all[:, j * LANES:(j + 1) * LANES]
        for gg in range(2):
            g = 2 * j + gg
            keep = low if gg == 0 else jnp.logical_not(low)
            kb = jnp.where(keep, kt, 0.0)
            vb = jnp.where(keep, vt, 0.0)
            ko = pltpu.roll(kb, HEAD_DIM, 1)
            vo = pltpu.roll(vb, HEAD_DIM, 1)
            k_lo, k_hi = (kb, ko) if gg == 0 else (ko, kb)
            v_lo, v_hi = (vb, vo) if gg == 0 else (vo, vb)
            kk = jnp.concatenate([k_lo, k_hi], axis=0).astype(BF16)
            vv = jnp.concatenate([v_lo, v_hi], axis=0).astype(BF16)
            for i in range(2):
                t = 2 * g + i
                s = lax.dot_general(q_tile(t), kk, (((1,), (1,)), ((), ())), preferred_element_type=F32)
                ps, inv = [], []
                for half in range(2):
                    sink = sink_ref[4 * g + 2 * i + half]
                    sh = jnp.where(valid, s[:, half * nk:(half + 1) * nk], NEG)
                    m = jnp.maximum(jnp.max(sh, axis=-1, keepdims=True), sink)
                    p = jnp.exp(sh - m)
                    den = jnp.sum(p, axis=-1, keepdims=True) + jnp.exp(sink - m)
                    ps.append(p.astype(BF16))
                    inv.append(1.0 / den)
                o = jnp.dot(jnp.concatenate(ps, axis=1), vv, preferred_element_type=F32)
                store(t, o * jnp.where(out_low, inv[0], inv[1]))


def _attn_prompt_body(sink_ref, q_ref, kvp_ref, kvc_ref, o_ref):
    n = pl.program_id(0)
    row = lax.broadcasted_iota(jnp.int32, (WINDOW, 2 * WINDOW), 0)
    col = lax.broadcasted_iota(jnp.int32, (WINDOW, 2 * WINDOW), 1)
    valid = (col > row) & (col <= row + WINDOW) & ((col >= WINDOW) | (n > 0))
    k_all = jnp.concatenate([kvp_ref[:, 0:KV_WIDTH], kvc_ref[:, 0:KV_WIDTH]], axis=0)
    v_all = jnp.concatenate([kvp_ref[:, KV_WIDTH:], kvc_ref[:, KV_WIDTH:]], axis=0)

    def store(t, o):
        o_ref[:, t * LANES:(t + 1) * LANES] = o.astype(o_ref.dtype)

    _attend(sink_ref, lambda t: q_ref[:, t * LANES:(t + 1) * LANES], k_all, v_all, valid, store)


def _attn_prompt(sinks, q, kv):
    S = q.shape[0]
    nb = S // WINDOW
    return pl.pallas_call(
        _attn_prompt_body,
        grid=(nb,),
        in_specs=[pl.BlockSpec(memory_space=pltpu.SMEM),
                  pl.BlockSpec((WINDOW, ATTN_WIDTH), lambda n: (n, 0)),
                  pl.BlockSpec((WINDOW, 2 * KV_WIDTH), lambda n: (jnp.maximum(n - 1, 0), 0)),
                  pl.BlockSpec((WINDOW, 2 * KV_WIDTH), lambda n: (n, 0))],
        out_specs=pl.BlockSpec((WINDOW, ATTN_WIDTH), lambda n: (n, 0)),
        out_shape=jax.ShapeDtypeStruct((S, ATTN_WIDTH), BF16),
        compiler_params=_params("arbitrary"),
        name="attn_prompt",
    )(sinks, q, kv, kv)


def _attn_sample_body(sink_ref, q_ref, kvn_ref, ck_ref, cv_ref, o_ref, kw_ref, vw_ref):
    bb, T = q_ref.shape[0], q_ref.shape[1]
    nbuf = ck_ref.shape[1]
    group = N_Q_HEADS // N_KV_HEADS
    half = HEAD_DIM
    nk = nbuf + T
    nkp = -(-nk // 16) * 16
    row = lax.broadcasted_iota(jnp.int32, (nkp, LANES), 0)
    tok = lax.broadcasted_iota(jnp.int32, (nkp, LANES), 1) & (T - 1)
    valid = (row > tok + nbuf - WINDOW) & (row <= tok + nbuf)
    pad = jnp.zeros((nkp - nk, KV_WIDTH), F32)
    low = lax.broadcasted_iota(jnp.int32, (T, LANES), 1) < half
    zeros = jnp.zeros((T, LANES), F32)
    sink = sink_ref[...]

    def one_seq(b, carry):
        kb, vb = ck_ref[b], cv_ref[b]
        kn, vn = kvn_ref[b][:, 0:KV_WIDTH], kvn_ref[b][:, KV_WIDTH:]
        kw_ref[b, 0:nbuf - T, :] = kb[T:nbuf]
        kw_ref[b, nbuf - T:nbuf, :] = kn
        vw_ref[b, 0:nbuf - T, :] = vb[T:nbuf]
        vw_ref[b, nbuf - T:nbuf, :] = vn
        k_all = jnp.concatenate([kb, kn, pad], axis=0).astype(BF16)
        v_all = jnp.concatenate([vb, vn, pad], axis=0).astype(BF16)
        qb = q_ref[b]
        q_rows = []
        for h in range(N_Q_HEADS):
            g = h // group
            piece = qb[:, (h // 2) * LANES:(h // 2 + 1) * LANES]
            if h % 2 != g % 2:
                piece = pltpu.roll(piece, half, 1)
            piece = jnp.where(low if g % 2 == 0 else jnp.logical_not(low), piece, 0.0)
            tiles = [zeros] * (KV_WIDTH // LANES)
            tiles[g // 2] = piece
            q_rows.append(jnp.concatenate(tiles, axis=1))
        qm = jnp.concatenate(q_rows, axis=0).astype(BF16)
        st = lax.dot_general(k_all, qm, (((1,), (1,)), ((), ())), preferred_element_type=F32)
        st = jnp.where(valid, st, NEG)
        m = jnp.maximum(jnp.max(st, axis=0, keepdims=True), sink)
        p = jnp.exp(st - m)
        den = jnp.sum(p, axis=0, keepdims=True) + jnp.exp(sink - m)
        pn = (p * (1.0 / den)).astype(BF16)
        o = lax.dot_general(pn, v_all, (((0,), (0,)), ((), ())), preferred_element_type=F32)
        for j in range(N_Q_HEADS // 2):
            g = (2 * j) // group
            c0 = (g // 2) * LANES
            first = o[(2 * j) * T:(2 * j + 1) * T, c0:c0 + LANES]
            second = o[(2 * j + 1) * T:(2 * j + 2) * T, c0:c0 + LANES]
            if g % 2 == 0:
                second = pltpu.roll(second, half, 1)
            else:
                first = pltpu.roll(first, half, 1)
            o_ref[b, :, j * LANES:(j + 1) * LANES] = jnp.where(low, first, second)
        return carry

    lax.fori_loop(0, bb, one_seq, 0, unroll=2)


def _attn_sample(sinks, q3, kvn3, ck, cv, bb):
    Bd, T = q3.shape[0], q3.shape[1]
    nbuf = ck.shape[1]
    assert nbuf == WINDOW and T == 8 and N_Q_HEADS * T == LANES and 2 * HEAD_DIM == LANES
    sink_lanes = jnp.repeat(sinks.astype(F32), T).reshape(1, LANES)
    blk = lambda r, w: pl.BlockSpec((bb, r, w), lambda i: (i, 0, 0))
    return pl.pallas_call(
        _attn_sample_body,
        grid=(Bd // bb,),
        in_specs=[pl.BlockSpec((1, LANES), lambda i: (0, 0)), blk(T, ATTN_WIDTH), blk(T, 2 * KV_WIDTH),
                  blk(nbuf, KV_WIDTH), blk(nbuf, KV_WIDTH)],
        out_specs=[blk(T, ATTN_WIDTH), blk(nbuf, KV_WIDTH), blk(nbuf, KV_WIDTH)],
        out_shape=[jax.ShapeDtypeStruct((Bd, T, ATTN_WIDTH), F32),
                   jax.ShapeDtypeStruct((Bd, nbuf, KV_WIDTH), F32),
                   jax.ShapeDtypeStruct((Bd, nbuf, KV_WIDTH), F32)],
        compiler_params=_params("arbitrary"),
        name="attn_sample",
    )(sink_lanes, q3, kvn3, ck, cv)


CONV_HALO = 32
CONV_ROWS = 32


def _conv_post(acc, gain, bias):
    return _silu(_ln(acc) * gain + bias)


def _conv_prompt_body(cur_ref, halo_ref, w_ref, b_ref, g_ref, beta_ref, o_ref, ext_ref, sh_ref):
    i = pl.program_id(0)
    tm = cur_ref.shape[0]
    ext_ref[0:CONV_HALO, :] = jnp.where(i > 0, halo_ref[...], 0.0)
    ext_ref[CONV_HALO:, :] = cur_ref[...]
    n_sh = sh_ref.shape[1]
    for r in range(1, SUBLANES):
        sh_ref[r - 1] = ext_ref[SUBLANES - r:SUBLANES - r + n_sh, :]
    for rr in range(tm // CONV_ROWS):
        acc = jnp.zeros((CONV_ROWS, CONV_CH), F32) + b_ref[...]
        for lag in range(CONV_WIDTH):
            a, r = divmod(lag, SUBLANES)
            base = rr * CONV_ROWS + CONV_HALO - SUBLANES * (a + 1)
            if r == 0:
                src = ext_ref[base + SUBLANES:base + SUBLANES + CONV_ROWS, :]
            else:
                src = sh_ref[r - 1, base:base + CONV_ROWS, :]
            acc = acc + w_ref[CONV_HIST - lag:CONV_HIST - lag + 1, :] * src
        y = _conv_post(acc, g_ref[...], beta_ref[...])
        o_ref[rr * CONV_ROWS:(rr + 1) * CONV_ROWS, :] = y.astype(o_ref.dtype)


def _conv_prompt(glu, w_dw, b_dw, gain, bias, tm):
    S = glu.shape[0]
    hb = tm // CONV_HALO
    vec = lambda: pl.BlockSpec((1, CONV_CH), lambda i: (0, 0))
    return pl.pallas_call(
        _conv_prompt_body,
        grid=(S // tm,),
        in_specs=[pl.BlockSpec((tm, CONV_CH), lambda i: (i, 0)),
                  pl.BlockSpec((CONV_HALO, CONV_CH), lambda i: (jnp.maximum(i * hb - 1, 0), 0)),
                  pl.BlockSpec((CONV_WIDTH, CONV_CH), lambda i: (0, 0)), vec(), vec(), vec()],
        out_specs=pl.BlockSpec((tm, CONV_CH), lambda i: (i, 0)),
        out_shape=jax.ShapeDtypeStruct((S, CONV_CH), BF16),
        scratch_shapes=[pltpu.VMEM((tm + CONV_HALO, CONV_CH), F32),
                        pltpu.VMEM((SUBLANES - 1, tm + CONV_HALO - SUBLANES, CONV_CH), F32)],
        compiler_params=_params("arbitrary"),
        name="conv_prompt",
    )(glu, glu, w_dw, b_dw.reshape(1, -1), gain.reshape(1, -1), bias.reshape(1, -1))


def _conv_sample_body(st_ref, glu_ref, w_ref, b_ref, g_ref, beta_ref, o_ref, ns_ref, ext_ref):
    T = glu_ref.shape[1]
    off = CONV_HALO - CONV_HIST
    ext_ref[:, off:CONV_HALO, :] = st_ref[...]
    ext_ref[:, CONV_HALO:CONV_HALO + T, :] = glu_ref[...]
    acc = jnp.zeros(o_ref.shape, F32) + b_ref[...]
    for j in range(CONV_WIDTH):
        acc = acc + w_ref[j:j + 1, :] * ext_ref[:, off + j:off + j + T, :]
    o_ref[...] = _conv_post(acc, g_ref[...], beta_ref[...])
    ns_ref[...] = ext_ref[:, off + T:CONV_HALO + T, :]


def _conv_sample(state, glu3, w_dw, b_dw, gain, bias, bb):
    Bd, T = glu3.shape[0], glu3.shape[1]
    vec = lambda: pl.BlockSpec((1, CONV_CH), lambda i: (0, 0))
    blk = lambda r: pl.BlockSpec((bb, r, CONV_CH), lambda i: (i, 0, 0))
    return pl.pallas_call(
        _conv_sample_body,
        grid=(Bd // bb,),
        in_specs=[blk(CONV_HIST), blk(T), pl.BlockSpec((CONV_WIDTH, CONV_CH), lambda i: (0, 0)), vec(), vec(), vec()],
        out_specs=[blk(T), blk(CONV_HIST)],
        out_shape=[jax.ShapeDtypeStruct((Bd, T, CONV_CH), F32),
                   jax.ShapeDtypeStruct((Bd, CONV_HIST, CONV_CH), F32)],
        scratch_shapes=[pltpu.VMEM((bb, CONV_HALO + T, CONV_CH), F32)],
        compiler_params=_params("arbitrary"),
        name="conv_sample",
    )(state, glu3, w_dw, b_dw.reshape(1, -1), gain.reshape(1, -1), bias.reshape(1, -1))


def _mix_body(alpha, attn_ref, conv_ref, ga_ref, gb_ref, x_ref, g1_ref, sh2_ref, sc2_ref, gain_ref, bias_ref,
              wa_ref, wc_ref, wo_ref, x1_ref, u2_ref):
    a = jnp.dot(attn_ref[...].astype(BF16), wa_ref[...], preferred_element_type=F32)
    c = jnp.dot(conv_ref[...].astype(BF16), wc_ref[...], preferred_element_type=F32)
    merged = ga_ref[...].astype(F32) * a + gb_ref[...].astype(F32) * c
    mix = jnp.dot(merged.astype(BF16), wo_ref[...], preferred_element_type=F32)
    x = x_ref[...]
    y = alpha * x + g1_ref[...][:, None, :] * mix.reshape(x.shape)
    x1 = _ln(y) * gain_ref[...] + bias_ref[...]
    x1_ref[...] = x1
    u2 = _ln(x1) * (1.0 + sc2_ref[...][:, None, :]) + sh2_ref[...][:, None, :]
    u2_ref[...] = u2.reshape(u2_ref.shape).astype(BF16)


def _mix(attn, conv, gates, x, mod, ln_gain, ln_bias, w_a, w_c, w_o, rows, alpha):
    vec = lambda: pl.BlockSpec((1, D_MODEL), lambda i: (0, 0))
    return pl.pallas_call(
        functools.partial(_mix_body, alpha),
        grid=(rows.n,),
        in_specs=[rows.row_spec(ATTN_WIDTH), rows.row_spec(CONV_CH), rows.row_spec(D_MODEL, 0), rows.row_spec(D_MODEL, 1),
                  rows.x_spec(), rows.mod_spec(2), rows.mod_spec(3), rows.mod_spec(4), vec(), vec(),
                  _resident((ATTN_WIDTH, D_MODEL)), _resident((CONV_CH, D_MODEL)), _resident((D_MODEL, D_MODEL))],
        out_specs=[rows.x_spec(), rows.row_spec(D_MODEL)],
        out_shape=[jax.ShapeDtypeStruct(x.shape, F32), jax.ShapeDtypeStruct((rows.rows, D_MODEL), BF16)],
        compiler_params=_params("arbitrary"),
        name="merge_out_ln1",
    )(attn, conv, gates, gates, x, mod, mod, mod, ln_gain.reshape(1, -1), ln_bias.reshape(1, -1), w_a, w_c, w_o)


FFN_UP_CHUNK = 256
FFN_DOWN_CHUNK = 512


def _ffn_body(alpha, u_ref, x1_ref, g2_ref, gain_ref, bias_ref, wg_ref, wu_ref, wd_ref, y_ref):
    k = pl.program_id(1)
    bb, tt, _ = y_ref.shape

    @pl.when(k == 0)
    def _():
        y_ref[...] = jnp.zeros(y_ref.shape, F32)

    u = u_ref[...]
    hs = []
    for c in range(wg_ref.shape[1] // FFN_UP_CHUNK):
        cols = slice(c * FFN_UP_CHUNK, (c + 1) * FFN_UP_CHUNK)
        g = jnp.dot(u, wg_ref[:, cols], preferred_element_type=F32)
        up = jnp.dot(u, wu_ref[:, cols], preferred_element_type=F32)
        hs.append((_silu(g) * up).astype(BF16))
    h = jnp.concatenate(hs, axis=1)
    for c in range(D_MODEL // FFN_DOWN_CHUNK):
        cols = slice(c * FFN_DOWN_CHUNK, (c + 1) * FFN_DOWN_CHUNK)
        d = jnp.dot(h, wd_ref[:, cols], preferred_element_type=F32)
        y_ref[:, :, cols] += d.reshape(bb, tt, FFN_DOWN_CHUNK)

    @pl.when(k == pl.num_programs(1) - 1)
    def _():
        y = alpha * x1_ref[...] + g2_ref[...][:, None, :] * y_ref[...]
        y_ref[...] = _ln(y) * gain_ref[...] + bias_ref[...]


def _ffn(u2, x1, mod, ln_gain, ln_bias, w_g, w_u, w_d, rows, tf, alpha):
    vec = lambda: pl.BlockSpec((1, D_MODEL), lambda i, k: (0, 0))
    return pl.pallas_call(
        functools.partial(_ffn_body, alpha),
        grid=(rows.n, D_FF // tf),
        in_specs=[rows.row_spec(D_MODEL), rows.x_spec(single=True), rows.mod_spec(5), vec(), vec(),
                  pl.BlockSpec((D_MODEL, tf), lambda i, k: (0, k)),
                  pl.BlockSpec((D_MODEL, tf), lambda i, k: (0, k)),
                  pl.BlockSpec((tf, D_MODEL), lambda i, k: (k, 0))],
        out_specs=rows.x_spec(),
        out_shape=jax.ShapeDtypeStruct(x1.shape, F32),
        compiler_params=_params("arbitrary", "arbitrary"),
        name="ffn_ln2",
    )(u2, x1, mod, ln_gain.reshape(1, -1), ln_bias.reshape(1, -1), w_g, w_u, w_d)


def _rope_tables(pos):
    half = HEAD_DIM // 2
    inv = ROPE_THETA ** (-jnp.arange(half, dtype=F32) / half)
    ang = pos.astype(F32)[:, None] * inv[None, :]
    cos, sin = jnp.cos(ang), jnp.sin(ang)
    reps = LANES // HEAD_DIM
    return jnp.tile(jnp.concatenate([cos, cos], axis=1), (1, reps)), jnp.tile(jnp.concatenate([-sin, sin], axis=1), (1, reps))


TM_LN = 512
TM_PROJ = 256
TM_MIX = 256
TM_FFN = 1024
TF_FFN = 512


def _dense_front(x, mod, w, pos_rows, q_dtype):
    B, T, _ = x.shape
    u = _ln_mod(x, mod, _Rows(B, T, TM_LN), 0, 1)
    cos_t, sin_t = _rope_tables(pos_rows)
    return _proj(u, w["in"], cos_t, sin_t, TM_PROJ, q_dtype)


def _dense_back(attn, conv, gates, x, mod, w, alpha):
    B, T, _ = x.shape
    x1, u2 = _mix(attn, conv, gates, x, mod, w["ln1_gain"], w["ln1_bias"], w["br_attn"], w["br_conv"], w["out"],
                  _Rows(B, T, TM_MIX), alpha)
    return _ffn(u2, x1, mod, w["ln2_gain"], w["ln2_bias"], w["ffn_gate"], w["ffn_up"], w["ffn_down"],
                _Rows(B, T, TM_FFN), TF_FFN, alpha)


def kernel(x_prompt, x_sample, c_prompt, c_sample, cache_k_win, cache_v_win, state_conv, w_ada, b_ada, w_in, attn_sinks, w_dw, b_dw, cn_gain, cn_bias, w_br_attn, w_br_conv, w_out, ln1_gain, ln1_bias, w_ffn_gate, w_ffn_up, w_ffn_down, ln2_gain, ln2_bias):
    depth = w_in.shape[0]
    alpha = (2 * depth) ** 0.25
    B, S, _ = x_prompt.shape
    Bd, T, _ = x_sample.shape
    nbuf = cache_k_win.shape[2]
    assert B == 1 and S % WINDOW == 0 and S >= WINDOW

    pos_p = jnp.arange(S, dtype=jnp.int32)
    pos_s = jnp.tile(PAST_LEN + jnp.arange(T, dtype=jnp.int32), Bd)

    n_c = B + Bd
    pad = (-n_c) % 8
    c_all = jnp.concatenate([c_prompt, c_sample, jnp.zeros((pad, D_MODEL), F32)], axis=0)

    yp, ys = x_prompt, x_sample
    outs = [[] for _ in range(6)]
    for l in range(depth):
        w = {
            "in": w_in[l].astype(BF16),
            "br_attn": w_br_attn[l].astype(BF16), "br_conv": w_br_conv[l].astype(BF16), "out": w_out[l].astype(BF16),
            "ffn_gate": w_ffn_gate[l].astype(BF16), "ffn_up": w_ffn_up[l].astype(BF16), "ffn_down": w_ffn_down[l].astype(BF16),
            "ln1_gain": ln1_gain[l], "ln1_bias": ln1_bias[l], "ln2_gain": ln2_gain[l], "ln2_bias": ln2_bias[l],
        }
        mod = _adaln(c_all, w_ada[l], b_ada[l])
        mod_p, mod_s = mod[0:B], mod[B:n_c]
        sinks = attn_sinks[l]

        q, kv, glu, gates = _dense_front(yp, mod_p, w, pos_p, BF16)
        attn = _attn_prompt(sinks, q, kv)
        conv = _conv_prompt(glu, w_dw[l], b_dw[l], cn_gain[l], cn_bias[l], 256)
        yp = _dense_back(attn, conv, gates, yp, mod_p, w, alpha)
        n = min(WINDOW, S)
        outs[0].append(kv[S - n:, 0:KV_WIDTH].reshape(B, n, N_KV_HEADS, HEAD_DIM))
        outs[1].append(kv[S - n:, KV_WIDTH:].reshape(B, n, N_KV_HEADS, HEAD_DIM))
        outs[2].append(glu[S - CONV_HIST:].reshape(B, CONV_HIST, CONV_CH))

        q, kv, glu, gates = _dense_front(ys, mod_s, w, pos_s, F32)
        attn3, kwin, vwin = _attn_sample(sinks, q.reshape(Bd, T, ATTN_WIDTH), kv.reshape(Bd, T, 2 * KV_WIDTH),
                                         cache_k_win[l].reshape(Bd, nbuf, KV_WIDTH), cache_v_win[l].reshape(Bd, nbuf, KV_WIDTH), 16)
        conv3, nstate = _conv_sample(state_conv[l], glu.reshape(Bd, T, CONV_CH), w_dw[l], b_dw[l], cn_gain[l], cn_bias[l], 16)
        ys = _dense_back(attn3.reshape(Bd * T, ATTN_WIDTH), conv3.reshape(Bd * T, CONV_CH), gates, ys, mod_s, w, alpha)
        outs[3].append(kwin.reshape(Bd, nbuf, N_KV_HEADS, HEAD_DIM))
        outs[4].append(vwin.reshape(Bd, nbuf, N_KV_HEADS, HEAD_DIM))
        outs[5].append(nstate)

    return (yp, ys) + tuple(jnp.stack(o) for o in outs)
```
